```python
import jax, jax.numpy as jnp
from jax import lax
import numpy as np

D_MODEL = 2048
BATCH = 4
SEQ = 4096
DEPTH = 2
DEC_BATCH = 16
DEC_SEQ = 32
PAST_LEN = 2048

CHUNK = 64
N_EVEN = (DEPTH + 1) // 2
N_ODD = DEPTH // 2
D_CONV_A = D_MODEL // 2
CONV_A_WIDTH = 31
D_POOL_B = D_MODEL // 2
POOL_WINDOWS = (2, 4, 8, 16)
N_POOL_GROUPS = len(POOL_WINDOWS)
POOL_GROUP = D_POOL_B // N_POOL_GROUPS
POOL_HIST = max(POOL_WINDOWS) - 1
D_IN_EVEN = 2 * D_CONV_A + D_POOL_B
D_CONV_C = D_MODEL
CONV_C_WIDTH = 3
N_MEM = 256
N_MEM_HEADS = 4
MEM_HEAD_DIM = D_MODEL // N_MEM_HEADS
N_EXPERTS = 32
TOP_K = 4
D_EXPERT = D_MODEL
SWIGLU_LIMIT = 7.0
SWIGLU_ALPHA = 1.702
MOE_BLOCK = 128
LN_EPS = 1e-5
DN_ALPHA = (2 * DEPTH) ** 0.25
DN_BETA = (8 * DEPTH) ** -0.25

kernel_name = 'hybrid_convpool_shortconv_moe_stream_step'


def layer_norm(x, g, b):
    xf = x.astype(jnp.float32)
    mu = jnp.mean(xf, axis=-1, keepdims=True)
    xc = xf - mu
    var = jnp.mean(xc * xc, axis=-1, keepdims=True)
    return (xc * lax.rsqrt(var + LN_EPS) * g + b).astype(x.dtype)


def causal_depthwise_conv(u, hist, w):
    full = jnp.concatenate([hist, u], axis=1)
    y = lax.conv_general_dilated(full, w[:, None, :].astype(full.dtype), window_strides=(1,), padding='VALID',
                                 dimension_numbers=('NWC', 'WIO', 'NWC'), feature_group_count=u.shape[-1])
    return y, full[:, -(w.shape[0] - 1):]


def multiscale_pool(v, hist, start_pos, w_mix, scale):
    bsz, s, _ = v.shape
    full = jnp.concatenate([hist, v], axis=1)
    cs = jnp.cumsum(full.astype(jnp.float32), axis=1)
    cs = jnp.concatenate([jnp.zeros((bsz, 1, D_POOL_B), jnp.float32), cs], axis=1)
    pos = start_pos + jnp.arange(s)
    vf = v.astype(jnp.float32)
    groups = []
    for g, w in enumerate(POOL_WINDOWS):
        sl = slice(g * POOL_GROUP, (g + 1) * POOL_GROUP)
        upper = cs[:, POOL_HIST + 1:POOL_HIST + 1 + s, sl]
        lower = cs[:, POOL_HIST + 1 - w:POOL_HIST + 1 - w + s, sl]
        cnt = jnp.minimum(pos + 1, w).astype(jnp.float32)[None, :, None]
        groups.append((upper - lower) / cnt - vf[..., sl])
    p = jnp.stack(groups, axis=2).astype(v.dtype)
    mixed = jnp.einsum('bsgc,gcd->bsgd', p, w_mix).reshape(bsz, s, D_POOL_B)
    return mixed * scale, full[:, -POOL_HIST:]


def even_mixer(x, start_pos, hist_a, hist_b, w_in, b_in, w_dw, b_dw, g_a, b_a, w_mix, scale, w_out):
    h = jnp.einsum('bsd,de->bse', x, w_in) + b_in
    a_val = h[..., :D_CONV_A]
    a_gate = h[..., D_CONV_A:2 * D_CONV_A]
    v_b = h[..., 2 * D_CONV_A:]
    u = a_val * jax.nn.sigmoid(a_gate)
    c, new_a = causal_depthwise_conv(u, hist_a, w_dw)
    a_out = jax.nn.silu(layer_norm(c + b_dw, g_a, b_a))
    b_out, new_b = multiscale_pool(v_b, hist_b, start_pos, w_mix, scale)
    y = jnp.einsum('bse,ed->bsd', jnp.concatenate([a_out, b_out], axis=-1), w_out)
    return y, new_a, new_b


def odd_mixer(x, hist_c, w_in, w_dw, w_out):
    h = jnp.einsum('bsd,de->bse', x, w_in)
    gate_b = h[..., :D_CONV_C]
    gate_c = h[..., D_CONV_C:2 * D_CONV_C]
    xt = h[..., 2 * D_CONV_C:]
    c, new_c = causal_depthwise_conv(gate_c * xt, hist_c, w_dw)
    return jnp.einsum('bse,ed->bsd', gate_b * c, w_out), new_c


def memory_cross_attention(x, k, v, w_q, w_o):
    q = jnp.einsum('bsd,dhk->bshk', x, w_q)
    scores = jnp.einsum('bshk,bmhk->bhsm', q, k).astype(jnp.float32) * (MEM_HEAD_DIM ** -0.5)
    p = jax.nn.softmax(scores, axis=-1).astype(v.dtype)
    o = jnp.einsum('bhsm,bmhk->bshk', p, v)
    return jnp.einsum('bshk,hkd->bsd', o, w_o)


def moe_ffn(h, w_router, b_router, w_gate_up, b_gate_up, w_down, b_down):
    bsz, s, d = h.shape
    t = bsz * s
    hf = h.reshape(t, d)
    logits = jnp.einsum('td,de->te', hf, w_router).astype(jnp.float32) + b_router.astype(jnp.float32)
    top_logits, top_idx = lax.top_k(logits, TOP_K)
    gates = jax.nn.softmax(top_logits, axis=-1)
    m = t * TOP_K
    flat_e = top_idx.reshape(m).astype(jnp.int32)
    order = jnp.argsort(flat_e).astype(jnp.int32)
    sorted_e = flat_e[order]
    counts = jnp.bincount(flat_e, length=N_EXPERTS).astype(jnp.int32)
    padded = (counts + MOE_BLOCK - 1) // MOE_BLOCK * MOE_BLOCK
    padded_end = jnp.cumsum(padded)
    padded_start = padded_end - padded
    start = jnp.cumsum(counts) - counts
    dest_sorted = (padded_start[sorted_e] + jnp.arange(m, dtype=jnp.int32) - start[sorted_e]).astype(jnp.int32)
    n_blocks = -(-m // MOE_BLOCK) + N_EXPERTS
    n_rows = n_blocks * MOE_BLOCK
    row_token = jnp.full((n_rows,), t, jnp.int32).at[dest_sorted].set(order // TOP_K)
    block_expert = jnp.minimum(
        jnp.searchsorted(padded_end, jnp.arange(n_blocks, dtype=jnp.int32) * MOE_BLOCK, side='right'),
        N_EXPERTS - 1).astype(jnp.int32)
    h_pad = jnp.concatenate([hf, jnp.zeros((1, d), hf.dtype)], axis=0)
    xb = h_pad[row_token].reshape(n_blocks, MOE_BLOCK, d)

    def expert_block(args):
        xblk, e = args
        gu = xblk @ w_gate_up[e] + b_gate_up[e]
        gate = jnp.minimum(gu[:, :D_EXPERT], SWIGLU_LIMIT)
        up = jnp.clip(gu[:, D_EXPERT:], -SWIGLU_LIMIT, SWIGLU_LIMIT)
        act = gate * jax.nn.sigmoid(SWIGLU_ALPHA * gate)
        return ((up + 1.0) * act) @ w_down[e] + b_down[e]

    yb = lax.map(expert_block, (xb, block_expert)).reshape(n_rows, d)
    dest = jnp.zeros((m,), jnp.int32).at[order].set(dest_sorted)
    y = yb[dest].reshape(t, TOP_K, d)
    out = jnp.einsum('tkd,tk->td', y, gates.astype(y.dtype))
    return out.reshape(bsz, s, d)


def trunk(x, start_pos, hist_a, hist_b, hist_c, mem_k, mem_v, params):
    (ln_g, ln_b, w_in_e, b_in_e, w_dw_a, b_dw_a, ln_a_g, ln_a_b, w_pool_mix, pool_scale, w_out_e,
     w_in_o, w_dw_c, w_out_o, w_q, w_o, w_router, b_router, w_gate_up, b_gate_up, w_down, b_down) = params
    new_a, new_b, new_c = [], [], []
    for l in range(DEPTH):
        if l % 2 == 0:
            e = l // 2
            mix, sa, sb = even_mixer(x, start_pos, hist_a[e], hist_b[e], w_in_e[e], b_in_e[e], w_dw_a[e],
                                     b_dw_a[e], ln_a_g[e], ln_a_b[e], w_pool_mix[e], pool_scale[e], w_out_e[e])
            new_a.append(sa)
            new_b.append(sb)
        else:
            o = l // 2
            mix, sc = odd_mixer(x, hist_c[o], w_in_o[o], w_dw_c[o], w_out_o[o])
            new_c.append(sc)
        x = layer_norm(DN_ALPHA * x + mix, ln_g[l, 0], ln_b[l, 0])
        x = layer_norm(DN_ALPHA * x + memory_cross_attention(x, mem_k[l], mem_v[l], w_q[l], w_o[l]),
                       ln_g[l, 1], ln_b[l, 1])
        x = layer_norm(DN_ALPHA * x + moe_ffn(x, w_router[l], b_router[l], w_gate_up[l], b_gate_up[l],
                                              w_down[l], b_down[l]), ln_g[l, 2], ln_b[l, 2])
    return x, jnp.stack(new_a), jnp.stack(new_b), jnp.stack(new_c)


def setup_inputs(seed: int = 0) -> dict:
    key = jax.random.key(seed)
    ks = iter(jax.random.split(key, 40))

    def nrm(shape, scale):
        return scale * jax.random.normal(next(ks), shape, jnp.float32)

    D = D_MODEL
    return {
        'x_prompt': nrm((BATCH, SEQ, D), 1.0),
        'x_sample': nrm((DEC_BATCH, DEC_SEQ, D), 1.0),
        'mem_prompt': nrm((BATCH, N_MEM, D), 1.0),
        'cache_mem_k': nrm((DEPTH, DEC_BATCH, N_MEM, N_MEM_HEADS, MEM_HEAD_DIM), 1.0),
        'cache_mem_v': nrm((DEPTH, DEC_BATCH, N_MEM, N_MEM_HEADS, MEM_HEAD_DIM), DN_BETA),
        'cache_conv_a': nrm((N_EVEN, DEC_BATCH, CONV_A_WIDTH - 1, D_CONV_A), 0.5),
        'cache_pool_b': nrm((N_EVEN, DEC_BATCH, POOL_HIST, D_POOL_B), 1.0),
        'cache_conv_c': nrm((N_ODD, DEC_BATCH, CONV_C_WIDTH - 1, D_CONV_C), 1.0),
        'ln_g': 1.0 + nrm((DEPTH, 3, D), 0.02),
        'ln_b': nrm((DEPTH, 3, D), 0.02),
        'w_in_e': nrm((N_EVEN, D, D_IN_EVEN), D ** -0.5),
        'b_in_e': nrm((N_EVEN, D_IN_EVEN), 0.02),
        'w_dw_a': nrm((N_EVEN, CONV_A_WIDTH, D_CONV_A), CONV_A_WIDTH ** -0.5),
        'b_dw_a': nrm((N_EVEN, D_CONV_A), 0.02),
        'ln_a_g': 1.0 + nrm((N_EVEN, D_CONV_A), 0.02),
        'ln_a_b': nrm((N_EVEN, D_CONV_A), 0.02),
        'w_pool_mix': nrm((N_EVEN, N_POOL_GROUPS, POOL_GROUP, POOL_GROUP), POOL_GROUP ** -0.5),
        'pool_scale': 1.0 + nrm((N_EVEN, D_POOL_B), 0.02),
        'w_out_e': nrm((N_EVEN, D_CONV_A + D_POOL_B, D), DN_BETA * (D_CONV_A + D_POOL_B) ** -0.5),
        'w_in_o': nrm((N_ODD, D, 3 * D_CONV_C), D ** -0.5),
        'w_dw_c': nrm((N_ODD, CONV_C_WIDTH, D_CONV_C), CONV_C_WIDTH ** -0.5),
        'w_out_o': nrm((N_ODD, D_CONV_C, D), DN_BETA * D_CONV_C ** -0.5),
        'w_q': nrm((DEPTH, D, N_MEM_HEADS, MEM_HEAD_DIM), D ** -0.5),
        'w_k': nrm((DEPTH, D, N_MEM_HEADS, MEM_HEAD_DIM), D ** -0.5),
        'w_v': nrm((DEPTH, D, N_MEM_HEADS, MEM_HEAD_DIM), DN_BETA * D ** -0.5),
        'w_o': nrm((DEPTH, N_MEM_HEADS, MEM_HEAD_DIM, D), DN_BETA * D ** -0.5),
        'w_router': nrm((DEPTH, D, N_EXPERTS), D ** -0.5),
        'b_router': nrm((DEPTH, N_EXPERTS), 0.01),
        'w_gate_up': nrm((DEPTH, N_EXPERTS, D, 2 * D_EXPERT), D ** -0.5),
        'b_gate_up': nrm((DEPTH, N_EXPERTS, 2 * D_EXPERT), 0.02),
        'w_down': nrm((DEPTH, N_EXPERTS, D_EXPERT, D), DN_BETA * D_EXPERT ** -0.5),
        'b_down': nrm((DEPTH, N_EXPERTS, D), 0.02),
    }


def reference(x_prompt, x_sample, mem_prompt, cache_mem_k, cache_mem_v, cache_conv_a, cache_pool_b,
              cache_conv_c, ln_g, ln_b, w_in_e, b_in_e, w_dw_a, b_dw_a, ln_a_g, ln_a_b, w_pool_mix,
              pool_scale, w_out_e, w_in_o, w_dw_c, w_out_o, w_q, w_k, w_v, w_o, w_router, b_router,
              w_gate_up, b_gate_up, w_down, b_down):
    assert x_sample.shape[1] <= CHUNK
    params = (ln_g, ln_b, w_in_e, b_in_e, w_dw_a, b_dw_a, ln_a_g, ln_a_b, w_pool_mix, pool_scale, w_out_e,
              w_in_o, w_dw_c, w_out_o, w_q, w_o, w_router, b_router, w_gate_up, b_gate_up, w_down, b_down)
    mem_k_prompt = jnp.einsum('bmd,ldhk->lbmhk', mem_prompt, w_k)
    mem_v_prompt = jnp.einsum('bmd,ldhk->lbmhk', mem_prompt, w_v)
    bp = x_prompt.shape[0]
    zero_a = jnp.zeros((N_EVEN, bp, CONV_A_WIDTH - 1, D_CONV_A), x_prompt.dtype)
    zero_b = jnp.zeros((N_EVEN, bp, POOL_HIST, D_POOL_B), x_prompt.dtype)
    zero_c = jnp.zeros((N_ODD, bp, CONV_C_WIDTH - 1, D_CONV_C), x_prompt.dtype)
    y_prompt, conv_a_prompt, pool_b_prompt, conv_c_prompt = trunk(
        x_prompt, 0, zero_a, zero_b, zero_c, mem_k_prompt, mem_v_prompt, params)
    y_sample, conv_a_sample, pool_b_sample, conv_c_sample = trunk(
        x_sample, PAST_LEN, cache_conv_a, cache_pool_b, cache_conv_c, cache_mem_k, cache_mem_v, params)
    return (y_prompt, y_sample, mem_k_prompt, mem_v_prompt, conv_a_prompt, pool_b_prompt, conv_c_prompt,
            conv_a_sample, pool_b_sample, conv_c_sample)
```

```python
import functools

import jax
import jax.numpy as jnp
from jax import lax
from jax.experimental import pallas as pl
from jax.experimental.pallas import tpu as pltpu

F32 = jnp.float32
BF16 = jnp.bfloat16
I32 = jnp.int32

POOL_WINDOWS = (2, 4, 8, 16)
TOP_K = 4
SWIGLU_LIMIT = 7.0
SWIGLU_ALPHA = 1.702
LN_EPS = 1e-5
PAST_LEN = 2048

SUBLANES = 8
LANES = 128
V7X_VMEM_BYTES = 64 * 1024 * 1024
VMEM_LIMIT = V7X_VMEM_BYTES - 8 * 1024 * 1024


def _params(*sem):
    return pltpu.CompilerParams(dimension_semantics=sem, vmem_limit_bytes=VMEM_LIMIT)


def _round_up(n, k):
    return -(-n // k) * k


def _tile(n, pref):
    if n <= pref:
        return n
    t = pref
    while n % t:
        t -= SUBLANES
    return t


def _ln(y, g, b):
    mu = jnp.mean(y, axis=-1, keepdims=True)
    yc = y - mu
    var = jnp.mean(yc * yc, axis=-1, keepdims=True)
    return yc * lax.rsqrt(var + LN_EPS) * g + b


def _resident(shape, index_map):
    return pl.BlockSpec(shape, index_map, pipeline_mode=pl.Buffered(1))


def _mm(x, ws, body, out_dtypes, row_ins=(), vec_ins=(), *, tm, tn, name):
    m, k = x.shape
    n = ws[0].shape[1]
    nw, nr, nv = len(ws), len(row_ins), len(vec_ins)
    assert m % tm == 0 and n % tn == 0

    def kern(*refs):
        x_ref = refs[0]
        w_refs = refs[1:1 + nw]
        r_refs = refs[1 + nw:1 + nw + nr]
        v_refs = refs[1 + nw + nr:1 + nw + nr + nv]
        o_refs = refs[1 + nw + nr + nv:]
        xv = x_ref[...]
        accs = [jnp.dot(xv, w[...], preferred_element_type=F32) for w in w_refs]
        outs = body(accs, [r[...] for r in r_refs], [v[...] for v in v_refs])
        for o_ref, o in zip(o_refs, outs):
            o_ref[...] = o.astype(o_ref.dtype)

    w_spec = _resident if n == tn else pl.BlockSpec
    in_specs = [pl.BlockSpec((tm, k), lambda j, i: (i, 0))]
    in_specs += [w_spec((k, tn), lambda j, i: (0, j)) for _ in ws]
    in_specs += [pl.BlockSpec((tm, tn), lambda j, i: (i, j)) for _ in row_ins]
    in_specs += [pl.BlockSpec((1, tn), lambda j, i: (0, j)) for _ in vec_ins]
    out_specs = [pl.BlockSpec((tm, tn), lambda j, i: (i, j)) for _ in out_dtypes]
    return pl.pallas_call(
        kern,
        grid=(n // tn, m // tm),
        in_specs=in_specs,
        out_specs=out_specs,
        out_shape=[jax.ShapeDtypeStruct((m, n), dt) for dt in out_dtypes],
        compiler_params=_params("arbitrary", "arbitrary"),
        name=name,
    )(x, *ws, *row_ins, *vec_ins)


def _res_ln_body(alpha):
    def body(accs, rows, vecs):
        y = _ln(alpha * rows[0] + accs[0], vecs[0], vecs[1])
        return [y, y]
    return body


def _even_mixer(u, v, hist_a, hist_b, w_dw, b_dw, g_a, b_a, w_mix, scale, *, bsz, seq, start_pos):
    c = u.shape[1]
    kw = w_dw.shape[0]
    ha, hb = kw - 1, max(POOL_WINDOWS) - 1
    hap, hbp = _round_up(ha, SUBLANES), _round_up(hb, SUBLANES)
    gc = c // len(POOL_WINDOWS)
    tq = _tile(seq, 256)
    nq = seq // tq
    rb = min(32, tq)
    assert tq >= hap and tq >= hbp and tq % rb == 0
    ha_pad = jnp.pad(hist_a, ((0, 0), (hap - ha, 0), (0, 0)))
    hb_pad = jnp.pad(hist_b, ((0, 0), (hbp - hb, 0), (0, 0)))

    def kern(u_ref, v_ref, ha_ref, hb_ref, wdw_ref, bdw_ref, ga_ref, ba_ref, wmix_ref, sc_ref,
             o_ref, wu, wv, rolled):
        j = pl.program_id(1)

        @pl.when(j == 0)
        def _():
            wu[0:hap, :] = ha_ref[0]
            wv[0:hbp, :] = hb_ref[0]

        wu[hap:hap + tq, :] = u_ref[...]
        wv[hbp:hbp + tq, :] = v_ref[...]

        win = wu[...]
        rolled[0] = win
        for b in range(1, SUBLANES):
            rolled[b] = pltpu.roll(win, b, 0)

        def conv_rows(r, carry):
            r0 = pl.multiple_of(r * rb, rb)
            acc = jnp.zeros((rb, c), F32)
            for s in range(kw):
                a, b = divmod(s, SUBLANES)
                tap = wdw_ref[kw - 1 - s:kw - s, :]
                acc = acc + tap * rolled[b, pl.ds(hap - SUBLANES * a + r0, rb), :]
            y = _ln(acc + bdw_ref[...], ga_ref[...], ba_ref[...])
            o_ref[pl.ds(r0, rb), 0:c] = (y * jax.nn.sigmoid(y)).astype(o_ref.dtype)
            return carry

        lax.fori_loop(0, tq // rb, conv_rows, 0)

        pos = start_pos + j * tq + lax.broadcasted_iota(I32, (tq, gc), 0)
        for g, w in enumerate(POOL_WINDOWS):
            xg = wv[:, g * gc:(g + 1) * gc]
            tot = xg
            step = 1
            while step < w:
                tot = tot + pltpu.roll(tot, step, 0)
                step *= 2
            cnt = jnp.minimum(pos + 1, w).astype(F32)
            p = tot[hbp:, :] / cnt - xg[hbp:, :]
            mixed = jnp.dot(p.astype(BF16), wmix_ref[g], preferred_element_type=F32)
            o_ref[:, c + g * gc:c + (g + 1) * gc] = (
                mixed * sc_ref[:, g * gc:(g + 1) * gc]).astype(o_ref.dtype)

        wu[0:hap, :] = wu[tq:tq + hap, :]
        wv[0:hbp, :] = wv[tq:tq + hbp, :]

    row = lambda b, j: (b * nq + j, 0)
    full2 = lambda b, j: (0, 0)
    return pl.pallas_call(
        kern,
        grid=(bsz, nq),
        in_specs=[
            pl.BlockSpec((tq, c), row),
            pl.BlockSpec((tq, c), row),
            pl.BlockSpec((1, hap, c), lambda b, j: (b, 0, 0)),
            pl.BlockSpec((1, hbp, c), lambda b, j: (b, 0, 0)),
            pl.BlockSpec(w_dw.shape, full2),
            pl.BlockSpec((1, c), full2),
            pl.BlockSpec((1, c), full2),
            pl.BlockSpec((1, c), full2),
            pl.BlockSpec(w_mix.shape, lambda b, j: (0, 0, 0)),
            pl.BlockSpec((1, c), full2),
        ],
        out_specs=pl.BlockSpec((tq, 2 * c), row),
        out_shape=jax.ShapeDtypeStruct((bsz * seq, 2 * c), BF16),
        scratch_shapes=[
            pltpu.VMEM((hap + tq, c), F32),
            pltpu.VMEM((hbp + tq, c), F32),
            pltpu.VMEM((SUBLANES, hap + tq, c), F32),
        ],
        compiler_params=_params("arbitrary", "arbitrary"),
        name="even_mixer",
    )(u, v, ha_pad, hb_pad, w_dw, b_dw, g_a, b_a, w_mix, scale)


def _odd_mixer(gb, gcx, hist_c, w_dw, *, bsz, seq):
    c = gb.shape[1]
    kw = w_dw.shape[0]
    hc = kw - 1
    hcp = _round_up(hc, SUBLANES)
    tq = _tile(seq, 256)
    nq = seq // tq
    assert tq >= hcp and kw <= SUBLANES
    hc_pad = jnp.pad(hist_c, ((0, 0), (hcp - hc, 0), (0, 0)))

    def kern(gb_ref, gc_ref, hc_ref, wdw_ref, o_ref, win):
        j = pl.program_id(1)

        @pl.when(j == 0)
        def _():
            win[0:hcp, :] = hc_ref[0]

        win[hcp:hcp + tq, :] = gc_ref[...]
        w = win[...]
        acc = wdw_ref[kw - 1:kw, :] * w[hcp:, :]
        for s in range(1, kw):
            acc = acc + wdw_ref[kw - 1 - s:kw - s, :] * pltpu.roll(w, s, 0)[hcp:, :]
        o_ref[...] = (gb_ref[...] * acc).astype(o_ref.dtype)
        win[0:hcp, :] = win[tq:tq + hcp, :]

    row = lambda b, j: (b * nq + j, 0)
    return pl.pallas_call(
        kern,
        grid=(bsz, nq),
        in_specs=[
            pl.BlockSpec((tq, c), row),
            pl.BlockSpec((tq, c), row),
            pl.BlockSpec((1, hcp, c), lambda b, j: (b, 0, 0)),
            pl.BlockSpec(w_dw.shape, lambda b, j: (0, 0)),
        ],
        out_specs=pl.BlockSpec((tq, c), row),
        out_shape=jax.ShapeDtypeStruct((bsz * seq, c), BF16),
        scratch_shapes=[pltpu.VMEM((hcp + tq, c), F32)],
        compiler_params=_params("arbitrary", "arbitrary"),
        name="odd_mixer",
    )(gb, gcx, hc_pad, w_dw)


def _attention(q, k, v, w_o, res, ln_g, ln_b, *, bsz, seq, n_heads, alpha):
    d = q.shape[1]
    n_mem = k.shape[1]
    dh = d // n_heads
    tq = _tile(seq, 256)
    nq = seq // tq
    sm_scale = dh ** -0.5

    def kern(q_ref, k_ref, v_ref, wo_ref, res_ref, g_ref, b_ref, o32_ref, o16_ref):
        acc = jnp.zeros((tq, d), F32)
        for h in range(n_heads):
            hs = slice(h * dh, (h + 1) * dh)
            s = lax.dot_general(q_ref[:, hs], k_ref[0, :, hs], (((1,), (1,)), ((), ())),
                                preferred_element_type=F32) * sm_scale
            s = s - jnp.max(s, axis=-1, keepdims=True)
            e = jnp.exp(s)
            p = (e / jnp.sum(e, axis=-1, keepdims=True)).astype(BF16)
            oh = jnp.dot(p, v_ref[0, :, hs], preferred_element_type=F32)
            acc = acc + jnp.dot(oh.astype(BF16), wo_ref[hs, :], preferred_element_type=F32)
        y = _ln(alpha * res_ref[...] + acc, g_ref[...], b_ref[...])
        o32_ref[...] = y
        o16_ref[...] = y.astype(BF16)

    row = lambda b, j: (b * nq + j, 0)
    full2 = lambda b, j: (0, 0)
    return pl.pallas_call(
        kern,
        grid=(bsz, nq),
        in_specs=[
            pl.BlockSpec((tq, d), row),
            pl.BlockSpec((1, n_mem, d), lambda b, j: (b, 0, 0)),
            pl.BlockSpec((1, n_mem, d), lambda b, j: (b, 0, 0)),
            _resident((d, d), full2),
            pl.BlockSpec((tq, d), row),
            pl.BlockSpec((1, d), full2),
            pl.BlockSpec((1, d), full2),
        ],
        out_specs=[pl.BlockSpec((tq, d), row), pl.BlockSpec((tq, d), row)],
        out_shape=[jax.ShapeDtypeStruct((bsz * seq, d), F32),
                   jax.ShapeDtypeStruct((bsz * seq, d), BF16)],
        compiler_params=_params("arbitrary", "arbitrary"),
        name="mem_attention",
    )(q, k, v, w_o, res, ln_g, ln_b)


def _router(x, w_r, b_r):
    t, d = x.shape
    e = w_r.shape[1]
    tm = _tile(t, 512)
    neg_inf = float("-inf")

    def kern(x_ref, w_ref, b_ref, idx_ref, gate_ref):
        logits = jnp.dot(x_ref[...], w_ref[...], precision=lax.Precision.HIGHEST,
                         preferred_element_type=F32) + b_ref[...]
        col = lax.broadcasted_iota(I32, (tm, e), 1)
        lane = lax.broadcasted_iota(I32, (tm, LANES), 1)
        idx_out = jnp.zeros((tm, LANES), I32)
        vals = []
        cur = logits
        for kk in range(TOP_K):
            mx = jnp.max(cur, axis=-1, keepdims=True)
            ix = jnp.min(jnp.where(cur == mx, col, e), axis=-1, keepdims=True)
            idx_out = jnp.where(lane == kk, ix, idx_out)
            vals.append(mx)
            cur = jnp.where(col == ix, neg_inf, cur)
        exps = [jnp.exp(val - vals[0]) for val in vals]
        den = exps[0]
        for ex in exps[1:]:
            den = den + ex
        gate_out = jnp.zeros((tm, LANES), F32)
        for kk in range(TOP_K):
            gate_out = jnp.where(lane == kk, exps[kk] / den, gate_out)
        idx_ref[...] = idx_out
        gate_ref[...] = gate_out

    return pl.pallas_call(
        kern,
        grid=(t // tm,),
        in_specs=[
            pl.BlockSpec((tm, d), lambda i: (i, 0)),
            pl.BlockSpec((d, e), lambda i: (0, 0)),
            pl.BlockSpec((1, e), lambda i: (0, 0)),
        ],
        out_specs=[pl.BlockSpec((tm, LANES), lambda i: (i, 0)),
                   pl.BlockSpec((tm, LANES), lambda i: (i, 0))],
        out_shape=[jax.ShapeDtypeStruct((t, LANES), I32), jax.ShapeDtypeStruct((t, LANES), F32)],
        compiler_params=_params("arbitrary"),
        name="router",
    )(x, w_r, b_r)


def _route_plan(top_idx, n_first, tm, n_experts):
    t = top_idx.shape[0]
    m = t * TOP_K
    flat_e = top_idx.reshape(m)
    order = jnp.argsort(flat_e, stable=True).astype(I32)
    sorted_e = flat_e[order]
    counts = jnp.bincount(flat_e, length=n_experts).astype(I32)
    padded = (counts + tm - 1) // tm * tm
    padded_end = jnp.cumsum(padded)
    padded_start = padded_end - padded
    start = jnp.cumsum(counts) - counts
    dest_sorted = (padded_start[sorted_e] + jnp.arange(m, dtype=I32) - start[sorted_e]).astype(I32)
    n_tiles = -(-m // tm) + n_experts
    row_token = jnp.full((n_tiles * tm,), n_first, I32).at[dest_sorted].set(order // TOP_K)
    dest = jnp.zeros((m,), I32).at[order].set(dest_sorted)
    n_used = (padded_end[-1] // tm).astype(I32)
    tile_ids = jnp.arange(n_tiles, dtype=I32)
    tile_e = jnp.minimum(jnp.searchsorted(padded_end, tile_ids * tm, side="right"),
                         n_experts - 1).astype(I32)
    last_e = tile_e[jnp.maximum(n_used - 1, 0)]
    tile_e = jnp.where(tile_ids < n_used, tile_e, last_e)
    row_token = row_token.reshape(n_tiles, 1, tm)
    tile_nfirst = jnp.sum(row_token[:, 0, :] < n_first, axis=1).astype(I32)
    return row_token, tile_e, tile_nfirst, n_used.reshape(1), dest


def _moe_ffn(x_first, x_second, plan, w_gate_up, b_gate_up, w_down, b_down, layer, *, tm, tf):
    row_token, tile_e, tile_nfirst, n_used, _ = plan
    n_first, d = x_first.shape
    n_layers, n_experts, _, de2 = w_gate_up.shape
    de = de2 // 2
    nf = de // tf
    n_tiles = row_token.shape[0]
    bgu = b_gate_up.reshape(n_layers, n_experts, 1, de2)
    bdn = b_down.reshape(n_layers, n_experts, 1, d)

    def chunk(i, f, nu):
        return jnp.where(i < nu[0], f, nf - 1)

    def kern(te, tnf, nu, rt_ref, xa_ref, xb_ref, wg_ref, wu_ref, bg_ref, bu_ref, wd_ref, bd_ref,
             o_ref, xf, xh, sem):
        i = pl.program_id(0)
        f = pl.program_id(1)
        used = i < nu[0]

        def row_copy(src_ref, tok, r):
            return pltpu.make_async_copy(src_ref.at[pl.ds(tok, 1)], xf.at[pl.ds(r, 1)], sem)

        @pl.when(used)
        def _():
            @pl.when(f == 0)
            def _():
                n_a = tnf[i]

                def issue_a(r, carry):
                    row_copy(xa_ref, rt_ref[0, 0, r], r).start()
                    return carry

                def issue_b(r, carry):
                    row_copy(xb_ref, rt_ref[0, 0, r] - n_first, r).start()
                    return carry

                def wait_one(r, carry):
                    row_copy(xa_ref, 0, r).wait()
                    return carry

                lax.fori_loop(0, n_a, issue_a, 0)
                lax.fori_loop(n_a, tm, issue_b, 0)
                lax.fori_loop(0, tm, wait_one, 0)
                xh[...] = xf[...].astype(BF16)

            x = xh[...]
            g = jnp.dot(x, wg_ref[...].astype(BF16), preferred_element_type=F32) + bg_ref[...]
            u = jnp.dot(x, wu_ref[...].astype(BF16), preferred_element_type=F32) + bu_ref[...]
            gate = jnp.minimum(g, SWIGLU_LIMIT)
            up = jnp.clip(u, -SWIGLU_LIMIT, SWIGLU_LIMIT)
            act = gate * jax.nn.sigmoid(SWIGLU_ALPHA * gate)
            hid = ((up + 1.0) * act).astype(BF16)
            contrib = jnp.dot(hid, wd_ref[...].astype(BF16), preferred_element_type=F32)

            @pl.when(f == 0)
            def _():
                o_ref[...] = contrib + bd_ref[...]

            @pl.when(f > 0)
            def _():
                o_ref[...] += contrib

        @pl.when(jnp.logical_not(used))
        def _():
            o_ref[...] = jnp.zeros_like(o_ref)

    grid_spec = pltpu.PrefetchScalarGridSpec(
        num_scalar_prefetch=3,
        grid=(n_tiles, nf),
        in_specs=[
            pl.BlockSpec((1, 1, tm), lambda i, f, te, tnf, nu: (i, 0, 0), memory_space=pltpu.SMEM),
            pl.BlockSpec(memory_space=pl.ANY),
            pl.BlockSpec(memory_space=pl.ANY),
            pl.BlockSpec((None, None, d, tf), lambda i, f, te, tnf, nu: (layer, te[i], 0, chunk(i, f, nu))),
            pl.BlockSpec((None, None, d, tf),
                         lambda i, f, te, tnf, nu: (layer, te[i], 0, nf + chunk(i, f, nu))),
            pl.BlockSpec((None, None, 1, tf), lambda i, f, te, tnf, nu: (layer, te[i], 0, chunk(i, f, nu))),
            pl.BlockSpec((None, None, 1, tf),
                         lambda i, f, te, tnf, nu: (layer, te[i], 0, nf + chunk(i, f, nu))),
            pl.BlockSpec((None, None, tf, d), lambda i, f, te, tnf, nu: (layer, te[i], chunk(i, f, nu), 0)),
            pl.BlockSpec((None, None, 1, d), lambda i, f, te, tnf, nu: (layer, te[i], 0, 0)),
        ],
        out_specs=pl.BlockSpec((tm, d), lambda i, f, te, tnf, nu: (i, 0)),
        scratch_shapes=[
            pltpu.VMEM((tm, d), F32),
            pltpu.VMEM((tm, d), BF16),
            pltpu.SemaphoreType.DMA(()),
        ],
    )
    return pl.pallas_call(
        kern,
        grid_spec=grid_spec,
        out_shape=jax.ShapeDtypeStruct((n_tiles * tm, d), F32),
        compiler_params=_params("arbitrary", "arbitrary"),
        name="moe_ffn",
    )(tile_e, tile_nfirst, n_used, row_token, x_first, x_second,
      w_gate_up, w_gate_up, bgu, bgu, w_down, bdn)


def _combine_ln(y_rows, dest, gates, res, ln_g, ln_b, *, alpha):
    t, d = res.shape
    tc = _tile(t, 128)
    n = t // tc
    dest3 = dest.reshape(n, 1, tc * TOP_K)

    def kern(dest_ref, gates_ref, y_ref, res_ref, g_ref, b_ref, o32_ref, o16_ref, buf, sem):
        def row_copy(src_row, kk, r):
            return pltpu.make_async_copy(y_ref.at[pl.ds(src_row, 1)], buf.at[kk, pl.ds(r, 1)], sem)

        def issue(r, carry):
            for kk in range(TOP_K):
                row_copy(dest_ref[0, 0, r * TOP_K + kk], kk, r).start()
            return carry

        def wait_one(r, carry):
            for kk in range(TOP_K):
                row_copy(0, kk, r).wait()
            return carry

        lax.fori_loop(0, tc, issue, 0)
        lax.fori_loop(0, tc, wait_one, 0)
        gts = gates_ref[...]
        acc = gts[:, 0:1] * buf[0]
        for kk in range(1, TOP_K):
            acc = acc + gts[:, kk:kk + 1] * buf[kk]
        y = _ln(alpha * res_ref[...] + acc, g_ref[...], b_ref[...])
        o32_ref[...] = y
        o16_ref[...] = y.astype(BF16)

    return pl.pallas_call(
        kern,
        grid=(n,),
        in_specs=[
            pl.BlockSpec((1, 1, tc * TOP_K), lambda i: (i, 0, 0), memory_space=pltpu.SMEM),
            pl.BlockSpec((tc, LANES), lambda i: (i, 0)),
            pl.BlockSpec(memory_space=pl.ANY),
            pl.BlockSpec((tc, d), lambda i: (i, 0)),
            pl.BlockSpec((1, d), lambda i: (0, 0)),
            pl.BlockSpec((1, d), lambda i: (0, 0)),
        ],
        out_specs=[pl.BlockSpec((tc, d), lambda i: (i, 0)), pl.BlockSpec((tc, d), lambda i: (i, 0))],
        out_shape=[jax.ShapeDtypeStruct((t, d), F32), jax.ShapeDtypeStruct((t, d), BF16)],
        scratch_shapes=[pltpu.VMEM((TOP_K, tc, d), F32), pltpu.SemaphoreType.DMA(())],
        compiler_params=_params("arbitrary"),
        name="moe_combine",
    )(dest3, gates, y_rows, res, ln_g, ln_b)


def kernel(x_prompt, x_sample, mem_prompt, cache_mem_k, cache_mem_v, cache_conv_a, cache_pool_b, cache_conv_c, ln_g, ln_b, w_in_e, b_in_e, w_dw_a, b_dw_a, ln_a_g, ln_a_b, w_pool_mix, pool_scale, w_out_e, w_in_o, w_dw_c, w_out_o, w_q, w_k, w_v, w_o, w_router, b_router, w_gate_up, b_gate_up, w_down, b_down):
    depth = ln_g.shape[0]
    d = x_prompt.shape[-1]
    n_heads, dh = w_q.shape[2], w_q.shape[3]
    n_experts = w_router.shape[-1]
    alpha = (2 * depth) ** 0.25
    bp, sp, _ = x_prompt.shape
    bs, ss, _ = x_sample.shape
    n_mem = mem_prompt.shape[1]
    c_a = cache_conv_a.shape[-1]
    d_c = cache_conv_c.shape[-1]
    t_p, t_s = bp * sp, bs * ss
    t_all = t_p + t_s
    moe_tm = _tile(t_all * TOP_K, 512)
    moe_tf = min(512, w_down.shape[2])

    mem_bf = mem_prompt.reshape(bp * n_mem, d).astype(BF16)
    kv_ws = []
    for l in range(depth):
        kv_ws += [w_k[l].reshape(d, d).astype(BF16), w_v[l].reshape(d, d).astype(BF16)]
    kv = _mm(mem_bf, kv_ws, lambda accs, rows, vecs: [a for a in accs for _ in range(2)],
             [F32, BF16] * (2 * depth), tm=_tile(bp * n_mem, 512), tn=min(512, d), name="mem_kv")
    mem_k32 = [kv[4 * l] for l in range(depth)]
    mem_k16 = [kv[4 * l + 1] for l in range(depth)]
    mem_v32 = [kv[4 * l + 2] for l in range(depth)]
    mem_v16 = [kv[4 * l + 3] for l in range(depth)]

    groups = [
        dict(b=bp, s=sp, start=0, x32=x_prompt.reshape(t_p, d),
             k=[a.reshape(bp, n_mem, d) for a in mem_k16], v=[a.reshape(bp, n_mem, d) for a in mem_v16],
             ha=jnp.zeros((depth, bp) + cache_conv_a.shape[2:], F32),
             hb=jnp.zeros((depth, bp) + cache_pool_b.shape[2:], F32),
             hc=jnp.zeros((depth, bp) + cache_conv_c.shape[2:], F32)),
        dict(b=bs, s=ss, start=PAST_LEN, x32=x_sample.reshape(t_s, d),
             k=[cache_mem_k[l].reshape(bs, n_mem, d).astype(BF16) for l in range(depth)],
             v=[cache_mem_v[l].reshape(bs, n_mem, d).astype(BF16) for l in range(depth)],
             ha=cache_conv_a, hb=cache_pool_b, hc=cache_conv_c),
    ]
    for grp in groups:
        grp["x16"] = grp["x32"].astype(BF16)
        grp["new_a"], grp["new_b"], grp["new_c"] = [], [], []

    def tail(hist, cur, bsz, seq):
        h = hist.shape[1]
        cur = cur.reshape(bsz, seq, cur.shape[-1])
        if seq >= h:
            return cur[:, seq - h:]
        return jnp.concatenate([hist, cur], axis=1)[:, -h:]

    for l in range(depth):
        vec = lambda a: a.reshape(1, -1)
        for grp in groups:
            bsz, seq = grp["b"], grp["s"]
            t_g = bsz * seq
            tm = _tile(t_g, 512)
            if l % 2 == 0:
                e = l // 2
                w_in = w_in_e[e].astype(BF16)
                b_in = b_in_e[e]
                ws = [w_in[:, 0:c_a], w_in[:, c_a:2 * c_a], w_in[:, 2 * c_a:]]
                bs_ = [vec(b_in[0:c_a]), vec(b_in[c_a:2 * c_a]), vec(b_in[2 * c_a:])]

                def in_body(accs, rows, vecs):
                    a_val = accs[0] + vecs[0]
                    a_gate = accs[1] + vecs[1]
                    return [a_val * jax.nn.sigmoid(a_gate), accs[2] + vecs[2]]

                u, vb = _mm(grp["x16"], ws, in_body, [F32, F32], vec_ins=bs_, tm=tm,
                            tn=min(512, c_a), name="in_proj_even")
                grp["new_a"].append(tail(grp["ha"][e], u, bsz, seq))
                grp["new_b"].append(tail(grp["hb"][e], vb, bsz, seq))
                mixed = _even_mixer(u, vb, grp["ha"][e], grp["hb"][e], w_dw_a[e], vec(b_dw_a[e]),
                                    vec(ln_a_g[e]), vec(ln_a_b[e]), w_pool_mix[e].astype(BF16),
                                    vec(pool_scale[e]), bsz=bsz, seq=seq, start_pos=grp["start"])
                w_out = w_out_e[e].astype(BF16)
            else:
                o = l // 2
                w_in = w_in_o[o].astype(BF16)
                ws = [w_in[:, 0:d_c], w_in[:, d_c:2 * d_c], w_in[:, 2 * d_c:]]
                gb, gcx = _mm(grp["x16"], ws, lambda accs, rows, vecs: [accs[0], accs[1] * accs[2]],
                              [F32, F32], tm=tm, tn=min(512, d_c), name="in_proj_odd")
                grp["new_c"].append(tail(grp["hc"][o], gcx, bsz, seq))
                mixed = _odd_mixer(gb, gcx, grp["hc"][o], w_dw_c[o], bsz=bsz, seq=seq)
                w_out = w_out_o[o].astype(BF16)

            x1_32, x1_16 = _mm(mixed, [w_out], _res_ln_body(alpha), [F32, BF16],
                               row_ins=[grp["x32"]], vec_ins=[vec(ln_g[l, 0]), vec(ln_b[l, 0])],
                               tm=tm, tn=d, name="out_proj_ln")
            (q16,) = _mm(x1_16, [w_q[l].reshape(d, d).astype(BF16)], lambda accs, rows, vecs: accs,
                         [BF16], tm=tm, tn=d, name="q_proj")
            grp["x32"], grp["x16"] = _attention(
                q16, grp["k"][l], grp["v"][l], w_o[l].reshape(d, d).astype(BF16), x1_32,
                vec(ln_g[l, 1]), vec(ln_b[l, 1]), bsz=bsz, seq=seq, n_heads=n_heads, alpha=alpha)

        idx_pad, gate_pad = [], []
        for grp in groups:
            ip, gp = _router(grp["x32"], w_router[l], vec(b_router[l]))
            idx_pad.append(ip)
            gate_pad.append(gp)
        top_idx = jnp.concatenate([ip[:, :TOP_K] for ip in idx_pad], axis=0)
        plan = _route_plan(top_idx, t_p, moe_tm, n_experts)
        y_rows = _moe_ffn(groups[0]["x32"], groups[1]["x32"], plan, w_gate_up, b_gate_up,
                          w_down, b_down, l, tm=moe_tm, tf=moe_tf)
        dest = plan[4]
        off = 0
        for grp, gp in zip(groups, gate_pad):
            t_g = grp["b"] * grp["s"]
            grp["x32"], grp["x16"] = _combine_ln(
                y_rows, dest[off * TOP_K:(off + t_g) * TOP_K], gp, grp["x32"],
                vec(ln_g[l, 2]), vec(ln_b[l, 2]), alpha=alpha)
            off += t_g

    gp_, gs_ = groups
    kv_shape = (bp, n_mem, n_heads, dh)
    return (gp_["x32"].reshape(bp, sp, d),
            gs_["x32"].reshape(bs, ss, d),
            jnp.stack([a.reshape(kv_shape) for a in mem_k32]),
            jnp.stack([a.reshape(kv_shape) for a in mem_v32]),
            jnp.stack(gp_["new_a"]), jnp.stack(gp_["new_b"]), jnp.stack(gp_["new_c"]),
            jnp.stack(gs_["new_a"]), jnp.stack(gs_["new_b"]), jnp.stack(gs_["new_c"]))
```

```python
import jax
import jax.numpy as jnp
from jax import lax
from jax.experimental import pallas as pl
from jax.experimental.pallas import tpu as pltpu

F32 = jnp.float32
BF16 = jnp.bfloat16
I32 = jnp.int32

POOL_WINDOWS = (2, 4, 8, 16)
TOP_K = 4
SWIGLU_LIMIT = 7.0
SWIGLU_ALPHA = 1.702
LN_EPS = 1e-5
PAST_LEN = 2048

SUBLANES = 8
LANES = 128
V7X_VMEM_BYTES = 64 * 1024 * 1024
VMEM_LIMIT = V7X_VMEM_BYTES - 6 * 1024 * 1024
DMA_UNROLL = 8


def _params(*sem):
    return pltpu.CompilerParams(dimension_semantics=sem, vmem_limit_bytes=VMEM_LIMIT)


def _round_up(n, k):
    return -(-n // k) * k


def _tile(n, pref):
    if n <= pref:
        return n
    t = pref
    while n % t:
        t -= SUBLANES
    return t


def _ln(y, g, b):
    mu = jnp.mean(y, axis=-1, keepdims=True)
    yc = y - mu
    var = jnp.mean(yc * yc, axis=-1, keepdims=True)
    return yc * lax.rsqrt(var + LN_EPS) * g + b


def _resident(shape, index_map):
    return pl.BlockSpec(shape, index_map, pipeline_mode=pl.Buffered(1))


def _mm(x, ws, body, out_dtypes, vec_ins=(), *, m, x_off=0, tm, tn, name):
    k = x.shape[1]
    n = ws[0].shape[1]
    nw, nv = len(ws), len(vec_ins)
    assert m % tm == 0 and n % tn == 0 and x_off % tm == 0
    xb = x_off // tm

    def kern(*refs):
        x_ref = refs[0]
        w_refs = refs[1:1 + nw]
        v_refs = refs[1 + nw:1 + nw + nv]
        o_refs = refs[1 + nw + nv:]
        xv = x_ref[...]
        accs = [jnp.dot(xv, w[...], preferred_element_type=F32) for w in w_refs]
        outs = body(accs, [v[...] for v in v_refs])
        for o_ref, o in zip(o_refs, outs):
            o_ref[...] = o.astype(o_ref.dtype)

    w_spec = _resident if n == tn else pl.BlockSpec
    in_specs = [pl.BlockSpec((tm, k), lambda j, i: (xb + i, 0))]
    in_specs += [w_spec((k, tn), lambda j, i: (0, j)) for _ in ws]
    in_specs += [pl.BlockSpec((1, tn), lambda j, i: (0, j)) for _ in vec_ins]
    out_specs = [pl.BlockSpec((tm, tn), lambda j, i: (i, j)) for _ in out_dtypes]
    return pl.pallas_call(
        kern,
        grid=(n // tn, m // tm),
        in_specs=in_specs,
        out_specs=out_specs,
        out_shape=[jax.ShapeDtypeStruct((m, n), dt) for dt in out_dtypes],
        compiler_params=_params("arbitrary", "arbitrary"),
        name=name,
    )(x, *ws, *vec_ins)


def _out_proj_ln_q(mixed, w_out, res, res_off, ln_g, ln_b, w_q, *, alpha, tm):
    m, k = mixed.shape
    d = w_out.shape[1]
    assert m % tm == 0 and res_off % tm == 0
    rb = res_off // tm

    def kern(x_ref, wo_ref, res_ref, g_ref, b_ref, wq_ref, o32_ref, q_ref):
        acc = jnp.dot(x_ref[...], wo_ref[...], preferred_element_type=F32)
        y = _ln(alpha * res_ref[...] + acc, g_ref[...], b_ref[...])
        o32_ref[...] = y
        q_ref[...] = jnp.dot(y.astype(BF16), wq_ref[...], preferred_element_type=F32).astype(BF16)

    return pl.pallas_call(
        kern,
        grid=(m // tm,),
        in_specs=[
            pl.BlockSpec((tm, k), lambda i: (i, 0)),
            _resident((k, d), lambda i: (0, 0)),
            pl.BlockSpec((tm, d), lambda i: (rb + i, 0)),
            pl.BlockSpec((1, d), lambda i: (0, 0)),
            pl.BlockSpec((1, d), lambda i: (0, 0)),
            _resident((d, d), lambda i: (0, 0)),
        ],
        out_specs=[pl.BlockSpec((tm, d), lambda i: (i, 0)), pl.BlockSpec((tm, d), lambda i: (i, 0))],
        out_shape=[jax.ShapeDtypeStruct((m, d), F32), jax.ShapeDtypeStruct((m, d), BF16)],
        compiler_params=_params("arbitrary"),
        name="out_proj_ln_q",
    )(mixed, w_out, res, ln_g, ln_b, w_q)


def _even_mixer(u, v, hist_a, hist_b, w_dw, b_dw, g_a, b_a, w_mix, scale, *, bsz, seq, start_pos):
    c = u.shape[1]
    kw = w_dw.shape[0]
    ha, hb = kw - 1, max(POOL_WINDOWS) - 1
    hap, hbp = _round_up(ha, SUBLANES), _round_up(hb, SUBLANES)
    gc = c // len(POOL_WINDOWS)
    tq = _tile(seq, 256)
    nq = seq // tq
    rb = min(32, tq)
    assert tq >= hap and tq >= hbp and tq % rb == 0
    ha_pad = jnp.pad(hist_a, ((0, 0), (hap - ha, 0), (0, 0)))
    hb_pad = jnp.pad(hist_b, ((0, 0), (hbp - hb, 0), (0, 0)))

    def kern(u_ref, v_ref, ha_ref, hb_ref, wdw_ref, bdw_ref, ga_ref, ba_ref, wmix_ref, sc_ref,
             o_ref, wu, wv, rolled):
        j = pl.program_id(1)

        @pl.when(j == 0)
        def _():
            wu[0:hap, :] = ha_ref[0]
            wv[0:hbp, :] = hb_ref[0]

        wu[hap:hap + tq, :] = u_ref[...]
        wv[hbp:hbp + tq, :] = v_ref[...]

        win = wu[...]
        rolled[0] = win
        for b in range(1, SUBLANES):
            rolled[b] = pltpu.roll(win, b, 0)

        def conv_rows(r, carry):
            r0 = pl.multiple_of(r * rb, rb)
            acc = jnp.zeros((rb, c), F32)
            for s in range(kw):
                a, b = divmod(s, SUBLANES)
                tap = wdw_ref[kw - 1 - s:kw - s, :]
                acc = acc + tap * rolled[b, pl.ds(hap - SUBLANES * a + r0, rb), :]
            y = _ln(acc + bdw_ref[...], ga_ref[...], ba_ref[...])
            o_ref[pl.ds(r0, rb), 0:c] = (y * jax.nn.sigmoid(y)).astype(o_ref.dtype)
            return carry

        lax.fori_loop(0, tq // rb, conv_rows, 0)

        pos = start_pos + j * tq + lax.broadcasted_iota(I32, (tq, gc), 0)
        for g, w in enumerate(POOL_WINDOWS):
            xg = wv[:, g * gc:(g + 1) * gc]
            tot = xg
            step = 1
            while step < w:
                tot = tot + pltpu.roll(tot, step, 0)
                step *= 2
            cnt = jnp.minimum(pos + 1, w).astype(F32)
            p = tot[hbp:, :] / cnt - xg[hbp:, :]
            mixed = jnp.dot(p.astype(BF16), wmix_ref[g], preferred_element_type=F32)
            o_ref[:, c + g * gc:c + (g + 1) * gc] = (
                mixed * sc_ref[:, g * gc:(g + 1) * gc]).astype(o_ref.dtype)

        wu[0:hap, :] = wu[tq:tq + hap, :]
        wv[0:hbp, :] = wv[tq:tq + hbp, :]

    row = lambda b, j: (b * nq + j, 0)
    full2 = lambda b, j: (0, 0)
    return pl.pallas_call(
        kern,
        grid=(bsz, nq),
        in_specs=[
            pl.BlockSpec((tq, c), row),
            pl.BlockSpec((tq, c), row),
            pl.BlockSpec((1, hap, c), lambda b, j: (b, 0, 0)),
            pl.BlockSpec((1, hbp, c), lambda b, j: (b, 0, 0)),
            pl.BlockSpec(w_dw.shape, full2),
            pl.BlockSpec((1, c), full2),
            pl.BlockSpec((1, c), full2),
            pl.BlockSpec((1, c), full2),
            pl.BlockSpec(w_mix.shape, lambda b, j: (0, 0, 0)),
            pl.BlockSpec((1, c), full2),
        ],
        out_specs=pl.BlockSpec((tq, 2 * c), row),
        out_shape=jax.ShapeDtypeStruct((bsz * seq, 2 * c), BF16),
        scratch_shapes=[
            pltpu.VMEM((hap + tq, c), F32),
            pltpu.VMEM((hbp + tq, c), F32),
            pltpu.VMEM((SUBLANES, hap + tq, c), F32),
        ],
        compiler_params=_params("arbitrary", "arbitrary"),
        name="even_mixer",
    )(u, v, ha_pad, hb_pad, w_dw, b_dw, g_a, b_a, w_mix, scale)


def _odd_mixer(gb, gcx, hist_c, w_dw, *, bsz, seq):
    c = gb.shape[1]
    kw = w_dw.shape[0]
    hc = kw - 1
    hcp = _round_up(hc, SUBLANES)
    tq = _tile(seq, 256)
    nq = seq // tq
    assert tq >= hcp and kw <= SUBLANES
    hc_pad = jnp.pad(hist_c, ((0, 0), (hcp - hc, 0), (0, 0)))

    def kern(gb_ref, gc_ref, hc_ref, wdw_ref, o_ref, win):
        j = pl.program_id(1)

        @pl.when(j == 0)
        def _():
            win[0:hcp, :] = hc_ref[0]

        win[hcp:hcp + tq, :] = gc_ref[...]
        w = win[...]
        acc = wdw_ref[kw - 1:kw, :] * w[hcp:, :]
        for s in range(1, kw):
            acc = acc + wdw_ref[kw - 1 - s:kw - s, :] * pltpu.roll(w, s, 0)[hcp:, :]
        o_ref[...] = (gb_ref[...] * acc).astype(o_ref.dtype)
        win[0:hcp, :] = win[tq:tq + hcp, :]

    row = lambda b, j: (b * nq + j, 0)
    return pl.pallas_call(
        kern,
        grid=(bsz, nq),
        in_specs=[
            pl.BlockSpec((tq, c), row),
            pl.BlockSpec((tq, c), row),
            pl.BlockSpec((1, hcp, c), lambda b, j: (b, 0, 0)),
            pl.BlockSpec(w_dw.shape, lambda b, j: (0, 0)),
        ],
        out_specs=pl.BlockSpec((tq, c), row),
        out_shape=jax.ShapeDtypeStruct((bsz * seq, c), BF16),
        scratch_shapes=[pltpu.VMEM((hcp + tq, c), F32)],
        compiler_params=_params("arbitrary", "arbitrary"),
        name="odd_mixer",
    )(gb, gcx, hc_pad, w_dw)


def _attention(q, k, v, w_o, res, ln_g, ln_b, *, bsz, seq, n_heads, alpha):
    d = q.shape[1]
    n_mem = k.shape[1]
    dh = d // n_heads
    tq = _tile(seq, 256)
    nq = seq // tq
    sm_scale = dh ** -0.5

    def kern(q_ref, k_ref, v_ref, wo_ref, res_ref, g_ref, b_ref, o32_ref):
        acc = jnp.zeros((tq, d), F32)
        for h in range(n_heads):
            hs = slice(h * dh, (h + 1) * dh)
            s = lax.dot_general(q_ref[:, hs], k_ref[0, :, hs], (((1,), (1,)), ((), ())),
                                preferred_element_type=F32) * sm_scale
            s = s - jnp.max(s, axis=-1, keepdims=True)
            e = jnp.exp(s)
            p = (e / jnp.sum(e, axis=-1, keepdims=True)).astype(BF16)
            oh = jnp.dot(p, v_ref[0, :, hs], preferred_element_type=F32)
            acc = acc + jnp.dot(oh.astype(BF16), wo_ref[hs, :], preferred_element_type=F32)
        o32_ref[...] = _ln(alpha * res_ref[...] + acc, g_ref[...], b_ref[...])

    row = lambda b, j: (b * nq + j, 0)
    full2 = lambda b, j: (0, 0)
    return pl.pallas_call(
        kern,
        grid=(bsz, nq),
        in_specs=[
            pl.BlockSpec((tq, d), row),
            pl.BlockSpec((1, n_mem, d), lambda b, j: (b, 0, 0)),
            pl.BlockSpec((1, n_mem, d), lambda b, j: (b, 0, 0)),
            _resident((d, d), full2),
            pl.BlockSpec((tq, d), row),
            pl.BlockSpec((1, d), full2),
            pl.BlockSpec((1, d), full2),
        ],
        out_specs=pl.BlockSpec((tq, d), row),
        out_shape=jax.ShapeDtypeStruct((bsz * seq, d), F32),
        compiler_params=_params("arbitrary", "arbitrary"),
        name="mem_attention",
    )(q, k, v, w_o, res, ln_g, ln_b)


def _router(x, w_r, b_r):
    t, d = x.shape
    e = w_r.shape[1]
    tm = _tile(t, 512)
    neg_inf = float("-inf")

    def kern(x_ref, w_ref, b_ref, idx_ref, gate_ref):
        xv = x_ref[...]
        wv = w_ref[...]
        x_hi = xv.astype(BF16)
        x_lo = (xv - x_hi.astype(F32)).astype(BF16)
        w_hi = wv.astype(BF16)
        w_lo = (wv - w_hi.astype(F32)).astype(BF16)
        logits = (jnp.dot(x_hi, w_hi, preferred_element_type=F32)
                  + jnp.dot(x_lo, w_hi, preferred_element_type=F32)
                  + jnp.dot(x_hi, w_lo, preferred_element_type=F32)) + b_ref[...]
        col = lax.broadcasted_iota(I32, (tm, e), 1)
        lane = lax.broadcasted_iota(I32, (tm, LANES), 1)
        idx_out = jnp.zeros((tm, LANES), I32)
        vals = []
        cur = logits
        for kk in range(TOP_K):
            mx = jnp.max(cur, axis=-1, keepdims=True)
            ix = jnp.min(jnp.where(cur == mx, col, e), axis=-1, keepdims=True)
            idx_out = jnp.where(lane == kk, ix, idx_out)
            vals.append(mx)
            cur = jnp.where(col == ix, neg_inf, cur)
        exps = [jnp.exp(val - vals[0]) for val in vals]
        den = exps[0]
        for ex in exps[1:]:
            den = den + ex
        gate_out = jnp.zeros((tm, LANES), F32)
        for kk in range(TOP_K):
            gate_out = jnp.where(lane == kk, exps[kk] / den, gate_out)
        idx_ref[...] = idx_out
        gate_ref[...] = gate_out

    return pl.pallas_call(
        kern,
        grid=(t // tm,),
        in_specs=[
            pl.BlockSpec((tm, d), lambda i: (i, 0)),
            pl.BlockSpec((d, e), lambda i: (0, 0)),
            pl.BlockSpec((1, e), lambda i: (0, 0)),
        ],
        out_specs=[pl.BlockSpec((tm, LANES), lambda i: (i, 0)),
                   pl.BlockSpec((tm, LANES), lambda i: (i, 0))],
        out_shape=[jax.ShapeDtypeStruct((t, LANES), I32), jax.ShapeDtypeStruct((t, LANES), F32)],
        compiler_params=_params("arbitrary"),
        name="router",
    )(x, w_r, b_r)


def _route_plan(top_idx, ts, sub, n_experts):
    t = top_idx.shape[0]
    m = t * TOP_K
    flat_e = top_idx.reshape(m)
    order = jnp.argsort(flat_e, stable=True).astype(I32)
    experts = jnp.arange(n_experts, dtype=I32)
    counts = jnp.sum((flat_e[:, None] == experts[None, :]).astype(I32), axis=0)
    start = jnp.cumsum(counts) - counts
    n_sup = (counts + ts - 1) // ts
    sup_end = jnp.cumsum(n_sup)
    n_used = sup_end[-1].astype(I32)
    n_tiles = m // ts + n_experts
    ids = jnp.arange(n_tiles, dtype=I32)
    e_raw = jnp.minimum(jnp.searchsorted(sup_end, ids, side="right"), n_experts - 1).astype(I32)
    used = ids < n_used
    tile_e = jnp.where(used, e_raw, e_raw[jnp.maximum(n_used - 1, 0)])
    j = ids - (sup_end[tile_e] - n_sup[tile_e])
    off = start[tile_e] + j * ts
    cnt = jnp.where(used, jnp.clip(counts[tile_e] - j * ts, 0, ts), 0)
    r = jnp.arange(ts, dtype=I32)
    ord_t = order[jnp.minimum(off[:, None] + r[None, :], m - 1)]
    valid = r[None, :] < cnt[:, None]
    src = jnp.where(valid, ord_t // TOP_K, 0).astype(I32)
    dst = jnp.where(valid, (ord_t % TOP_K) * (t + sub) + ord_t // TOP_K, (t + r % sub)[None, :]).astype(I32)
    n_sub = ((cnt + sub - 1) // sub).astype(I32)
    return (src.reshape(n_tiles, 1, ts), dst.reshape(n_tiles, 1, ts), tile_e.astype(I32), n_sub,
            n_used.reshape(1))


def _moe_ffn(x_all, plan, w_gate_up, b_gate_up, w_down, b_down, layer, *, ts, sub, tf):
    src, dst, tile_e, tile_nsub, n_used = plan
    t, d = x_all.shape
    n_layers, n_experts, _, de2 = w_gate_up.shape
    de = de2 // 2
    nf = de // tf
    n_tiles = src.shape[0]
    nsubs = ts // sub
    assert ts % sub == 0 and sub % DMA_UNROLL == 0 and de % tf == 0
    bgu = b_gate_up.reshape(n_layers, n_experts, 1, de2)
    bdn = b_down.reshape(n_layers, n_experts, 1, d)
    sl = SUBLANES
    sl_shift = sl.bit_length() - 1
    assert t % sl == 0 and sub % sl == 0 and 1 << sl_shift == sl
    x3 = x_all.reshape(t // sl, sl, d)
    y_rows = TOP_K * (t + sub)

    def chunk(i, f, nu):
        return jnp.where(i < nu[0], f, nf - 1)

    def kern(te, tns, nu, src_ref, srcn_ref, dst_ref, x_ref, wg_ref, wu_ref, bg_ref, bu_ref, wd_ref,
             bd_ref, y_ref, xf, xh, acc, gsem, ssem):
        i = pl.program_id(0)
        f = pl.program_id(1)
        n_used_ = nu[0]
        nsub_i = tns[i]

        def gather_rows(idx_ref, n_blocks):
            for s in range(nsubs):
                @pl.when(s < n_blocks)
                def _():
                    def body(g, carry):
                        blk = s * (sub // sl) + g
                        for q in range(sl):
                            tok = idx_ref[0, 0, blk * sl + q]
                            pltpu.make_async_copy(x_ref.at[tok >> sl_shift, pl.ds(tok & (sl - 1), 1)],
                                                  xf.at[blk, pl.ds(q, 1)], gsem).start()
                        return carry
                    lax.fori_loop(0, sub // sl, body, 0)

        def wait_gather(n_blocks):
            for s in range(nsubs):
                @pl.when(s < n_blocks)
                def _():
                    pltpu.make_async_copy(x_ref.at[pl.ds(0, sub // sl)],
                                          xf.at[pl.ds(s * (sub // sl), sub // sl)], gsem).wait()

        def scatter_rows(n_blocks):
            for s in range(nsubs):
                @pl.when(s < n_blocks)
                def _():
                    def body(g, carry):
                        blk = s * (sub // sl) + g
                        for q in range(sl):
                            row = dst_ref[0, 0, blk * sl + q]
                            pltpu.make_async_copy(acc.at[blk, pl.ds(q, 1)],
                                                  y_ref.at[row >> sl_shift, pl.ds(row & (sl - 1), 1)],
                                                  ssem).start()
                        return carry
                    lax.fori_loop(0, sub // sl, body, 0)

        def wait_scatter(n_blocks):
            for s in range(nsubs):
                @pl.when(s < n_blocks)
                def _():
                    pltpu.make_async_copy(acc.at[pl.ds(s * (sub // sl), sub // sl)],
                                          y_ref.at[pl.ds(0, sub // sl)], ssem).wait()

        @pl.when(jnp.logical_and(i == 0, f == 0))
        def _():
            acc[...] = jnp.zeros_like(acc)
            spare = [pltpu.make_async_copy(
                acc.at[pl.ds(0, sub // sl)],
                y_ref.at[pl.ds((kk * (t + sub) + t) // sl, sub // sl)], ssem) for kk in range(TOP_K)]
            for cp in spare:
                cp.start()
            for cp in spare:
                cp.wait()

        @pl.when(i < n_used_)
        def _():
            @pl.when(f == 0)
            def _():
                @pl.when(i == 0)
                def _():
                    gather_rows(src_ref, nsub_i)

                wait_gather(nsub_i)
                for s in range(nsubs):
                    @pl.when(s < nsub_i)
                    def _():
                        blks = slice(s * (sub // sl), (s + 1) * (sub // sl))
                        xh[s * sub:(s + 1) * sub, :] = xf[blks].reshape(sub, d).astype(BF16)

                @pl.when(i + 1 < n_used_)
                def _():
                    gather_rows(srcn_ref, tns[jnp.minimum(i + 1, n_tiles - 1)])

            for s in range(nsubs):
                @pl.when(s < nsub_i)
                def _():
                    blks = slice(s * (sub // sl), (s + 1) * (sub // sl))
                    x = xh[s * sub:(s + 1) * sub, :]
                    g = jnp.dot(x, wg_ref[...].astype(BF16), preferred_element_type=F32) + bg_ref[...]
                    u = jnp.dot(x, wu_ref[...].astype(BF16), preferred_element_type=F32) + bu_ref[...]
                    gate = jnp.minimum(g, SWIGLU_LIMIT)
                    up = jnp.clip(u, -SWIGLU_LIMIT, SWIGLU_LIMIT)
                    act = gate * jax.nn.sigmoid(SWIGLU_ALPHA * gate)
                    hid = ((up + 1.0) * act).astype(BF16)
                    contrib = jnp.dot(hid, wd_ref[...].astype(BF16), preferred_element_type=F32)

                    @pl.when(f == 0)
                    def _():
                        if s == 0:
                            @pl.when(i > 0)
                            def _():
                                wait_scatter(tns[jnp.maximum(i - 1, 0)])
                        acc[blks] = (contrib + bd_ref[...]).reshape(sub // sl, sl, d)

                    @pl.when(f > 0)
                    def _():
                        acc[blks] += contrib.reshape(sub // sl, sl, d)

            @pl.when(f == nf - 1)
            def _():
                scatter_rows(nsub_i)

                @pl.when(i == n_used_ - 1)
                def _():
                    wait_scatter(nsub_i)

    def tile_map(i, f, te, tns, nu):
        return (i, 0, 0)

    def next_tile_map(i, f, te, tns, nu):
        return (jnp.minimum(i + 1, n_tiles - 1), 0, 0)

    grid_spec = pltpu.PrefetchScalarGridSpec(
        num_scalar_prefetch=3,
        grid=(n_tiles, nf),
        in_specs=[
            pl.BlockSpec((1, 1, ts), tile_map, memory_space=pltpu.SMEM),
            pl.BlockSpec((1, 1, ts), next_tile_map, memory_space=pltpu.SMEM),
            pl.BlockSpec((1, 1, ts), tile_map, memory_space=pltpu.SMEM),
            pl.BlockSpec(memory_space=pl.ANY),
            pl.BlockSpec((None, None, d, tf), lambda i, f, te, tns, nu: (layer, te[i], 0, chunk(i, f, nu))),
            pl.BlockSpec((None, None, d, tf),
                         lambda i, f, te, tns, nu: (layer, te[i], 0, nf + chunk(i, f, nu))),
            pl.BlockSpec((None, None, 1, tf), lambda i, f, te, tns, nu: (layer, te[i], 0, chunk(i, f, nu))),
            pl.BlockSpec((None, None, 1, tf),
                         lambda i, f, te, tns, nu: (layer, te[i], 0, nf + chunk(i, f, nu))),
            pl.BlockSpec((None, None, tf, d), lambda i, f, te, tns, nu: (layer, te[i], chunk(i, f, nu), 0)),
            pl.BlockSpec((None, None, 1, d), lambda i, f, te, tns, nu: (layer, te[i], 0, 0)),
        ],
        out_specs=pl.BlockSpec(memory_space=pl.ANY),
        scratch_shapes=[
            pltpu.VMEM((ts // sl, sl, d), F32),
            pltpu.VMEM((ts, d), BF16),
            pltpu.VMEM((ts // sl, sl, d), F32),
            pltpu.SemaphoreType.DMA(()),
            pltpu.SemaphoreType.DMA(()),
        ],
    )
    y = pl.pallas_call(
        kern,
        grid_spec=grid_spec,
        out_shape=jax.ShapeDtypeStruct((y_rows // sl, sl, d), F32),
        compiler_params=_params("arbitrary", "arbitrary"),
        name="moe_ffn",
    )(tile_e, tile_nsub, n_used, src, src, dst, x3, w_gate_up, w_gate_up, bgu, bgu, w_down, bdn)
    return y.reshape(TOP_K, t + sub, d)


def _combine_ln(y, gates, res, ln_g, ln_b, *, off, m, alpha):
    d = res.shape[1]
    tc = _tile(m, 256)
    assert m % tc == 0 and off % tc == 0
    ob = off // tc

    def kern(*refs):
        y_refs = refs[:TOP_K]
        gates_ref, res_ref, g_ref, b_ref, o32_ref, o16_ref = refs[TOP_K:]
        gts = gates_ref[...]
        acc = gts[:, 0:1] * y_refs[0][...]
        for kk in range(1, TOP_K):
            acc = acc + gts[:, kk:kk + 1] * y_refs[kk][...]
        out = _ln(alpha * res_ref[...] + acc, g_ref[...], b_ref[...])
        o32_ref[...] = out
        o16_ref[...] = out.astype(BF16)

    def plane(kk):
        return pl.BlockSpec((None, tc, d), lambda i: (kk, ob + i, 0))

    return pl.pallas_call(
        kern,
        grid=(m // tc,),
        in_specs=[plane(kk) for kk in range(TOP_K)] + [
            pl.BlockSpec((tc, LANES), lambda i: (ob + i, 0)),
            pl.BlockSpec((tc, d), lambda i: (ob + i, 0)),
            pl.BlockSpec((1, d), lambda i: (0, 0)),
            pl.BlockSpec((1, d), lambda i: (0, 0)),
        ],
        out_specs=[pl.BlockSpec((tc, d), lambda i: (i, 0)), pl.BlockSpec((tc, d), lambda i: (i, 0))],
        out_shape=[jax.ShapeDtypeStruct((m, d), F32), jax.ShapeDtypeStruct((m, d), BF16)],
        compiler_params=_params("arbitrary"),
        name="moe_combine",
    )(*([y] * TOP_K), gates, res, ln_g, ln_b)


def kernel(x_prompt, x_sample, mem_prompt, cache_mem_k, cache_mem_v, cache_conv_a, cache_pool_b, cache_conv_c, ln_g, ln_b, w_in_e, b_in_e, w_dw_a, b_dw_a, ln_a_g, ln_a_b, w_pool_mix, pool_scale, w_out_e, w_in_o, w_dw_c, w_out_o, w_q, w_k, w_v, w_o, w_router, b_router, w_gate_up, b_gate_up, w_down, b_down):
    depth = ln_g.shape[0]
    d = x_prompt.shape[-1]
    n_heads, dh = w_q.shape[2], w_q.shape[3]
    n_experts = w_router.shape[-1]
    alpha = (2 * depth) ** 0.25
    bp, sp, _ = x_prompt.shape
    bs, ss, _ = x_sample.shape
    n_mem = mem_prompt.shape[1]
    c_a = cache_conv_a.shape[-1]
    d_c = cache_conv_c.shape[-1]
    t_p, t_s = bp * sp, bs * ss
    t_all = t_p + t_s
    moe_sub = 512
    moe_ts = 2 * moe_sub
    moe_tf = min(512, w_down.shape[2])

    mem_bf = mem_prompt.reshape(bp * n_mem, d).astype(BF16)
    kv_ws = []
    for l in range(depth):
        kv_ws += [w_k[l].reshape(d, d).astype(BF16), w_v[l].reshape(d, d).astype(BF16)]
    kv = _mm(mem_bf, kv_ws, lambda accs, vecs: [a for a in accs for _ in range(2)],
             [F32, BF16] * (2 * depth), m=bp * n_mem, tm=_tile(bp * n_mem, 512), tn=min(512, d),
             name="mem_kv")
    mem_k32 = [kv[4 * l] for l in range(depth)]
    mem_k16 = [kv[4 * l + 1] for l in range(depth)]
    mem_v32 = [kv[4 * l + 2] for l in range(depth)]
    mem_v16 = [kv[4 * l + 3] for l in range(depth)]

    groups = [
        dict(b=bp, s=sp, start=0, x32=(x_prompt.reshape(t_p, d), 0),
             x16=(x_prompt.reshape(t_p, d).astype(BF16), 0),
             k=[a.reshape(bp, n_mem, d) for a in mem_k16], v=[a.reshape(bp, n_mem, d) for a in mem_v16],
             ha=jnp.zeros((depth, bp) + cache_conv_a.shape[2:], F32),
             hb=jnp.zeros((depth, bp) + cache_pool_b.shape[2:], F32),
             hc=jnp.zeros((depth, bp) + cache_conv_c.shape[2:], F32)),
        dict(b=bs, s=ss, start=PAST_LEN, x32=(x_sample.reshape(t_s, d), 0),
             x16=(x_sample.reshape(t_s, d).astype(BF16), 0),
             k=[cache_mem_k[l].reshape(bs, n_mem, d).astype(BF16) for l in range(depth)],
             v=[cache_mem_v[l].reshape(bs, n_mem, d).astype(BF16) for l in range(depth)],
             ha=cache_conv_a, hb=cache_pool_b, hc=cache_conv_c),
    ]
    for grp in groups:
        grp["new_a"], grp["new_b"], grp["new_c"] = [], [], []

    def tail(hist, cur, bsz, seq):
        h = hist.shape[1]
        cur = cur.reshape(bsz, seq, cur.shape[-1])
        if seq >= h:
            return cur[:, seq - h:]
        return jnp.concatenate([hist, cur], axis=1)[:, -h:]

    vec = lambda a: a.reshape(1, -1)
    outs = None
    for l in range(depth):
        x2 = []
        for grp in groups:
            bsz, seq = grp["b"], grp["s"]
            t_g = bsz * seq
            tm = _tile(t_g, 512)
            x16, x16_off = grp["x16"]
            x32, x32_off = grp["x32"]
            if l % 2 == 0:
                e = l // 2
                w_in = w_in_e[e].astype(BF16)
                b_in = b_in_e[e]
                ws = [w_in[:, 0:c_a], w_in[:, c_a:2 * c_a], w_in[:, 2 * c_a:]]
                bs_ = [vec(b_in[0:c_a]), vec(b_in[c_a:2 * c_a]), vec(b_in[2 * c_a:])]

                def in_body(accs, vecs):
                    a_val = accs[0] + vecs[0]
                    a_gate = accs[1] + vecs[1]
                    return [a_val * jax.nn.sigmoid(a_gate), accs[2] + vecs[2]]

                u, vb = _mm(x16, ws, in_body, [F32, F32], vec_ins=bs_, m=t_g, x_off=x16_off, tm=tm,
                            tn=min(512, c_a), name="in_proj_even")
                grp["new_a"].append(tail(grp["ha"][e], u, bsz, seq))
                grp["new_b"].append(tail(grp["hb"][e], vb, bsz, seq))
                mixed = _even_mixer(u, vb, grp["ha"][e], grp["hb"][e], w_dw_a[e], vec(b_dw_a[e]),
                                    vec(ln_a_g[e]), vec(ln_a_b[e]), w_pool_mix[e].astype(BF16),
                                    vec(pool_scale[e]), bsz=bsz, seq=seq, start_pos=grp["start"])
                w_out = w_out_e[e].astype(BF16)
            else:
                o = l // 2
                w_in = w_in_o[o].astype(BF16)
                ws = [w_in[:, 0:d_c], w_in[:, d_c:2 * d_c], w_in[:, 2 * d_c:]]
                gb, gcx = _mm(x16, ws, lambda accs, vecs: [accs[0], accs[1] * accs[2]], [F32, F32],
                              m=t_g, x_off=x16_off, tm=tm, tn=min(512, d_c), name="in_proj_odd")
                grp["new_c"].append(tail(grp["hc"][o], gcx, bsz, seq))
                mixed = _odd_mixer(gb, gcx, grp["hc"][o], w_dw_c[o], bsz=bsz, seq=seq)
                w_out = w_out_o[o].astype(BF16)

            x1_32, q16 = _out_proj_ln_q(mixed, w_out, x32, x32_off, vec(ln_g[l, 0]), vec(ln_b[l, 0]),
                                        w_q[l].reshape(d, d).astype(BF16), alpha=alpha, tm=tm)
            x2.append(_attention(q16, grp["k"][l], grp["v"][l], w_o[l].reshape(d, d).astype(BF16),
                                 x1_32, vec(ln_g[l, 1]), vec(ln_b[l, 1]), bsz=bsz, seq=seq,
                                 n_heads=n_heads, alpha=alpha))

        x2_all = jnp.concatenate(x2, axis=0)
        idx_pad, gates = _router(x2_all, w_router[l], vec(b_router[l]))
        plan = _route_plan(idx_pad[:, :TOP_K], moe_ts, moe_sub, n_experts)
        y = _moe_ffn(x2_all, plan, w_gate_up, b_gate_up, w_down, b_down, l,
                     ts=moe_ts, sub=moe_sub, tf=moe_tf)
        if l + 1 < depth:
            x3_32, x3_16 = _combine_ln(y, gates, x2_all, vec(ln_g[l, 2]), vec(ln_b[l, 2]),
                                       off=0, m=t_all, alpha=alpha)
            off = 0
            for grp in groups:
                grp["x32"], grp["x16"] = (x3_32, off), (x3_16, off)
                off += grp["b"] * grp["s"]
        else:
            outs, off = [], 0
            for grp in groups:
                t_g = grp["b"] * grp["s"]
                outs.append(_combine_ln(y, gates, x2_all, vec(ln_g[l, 2]), vec(ln_b[l, 2]),
                                        off=off, m=t_g, alpha=alpha)[0])
                off += t_g

    gp_, gs_ = groups
    kv_shape = (bp, n_mem, n_heads, dh)
    return (outs[0].reshape(bp, sp, d),
            outs[1].reshape(bs, ss, d),
            jnp.stack([a.reshape(kv_shape) for a in mem_k32]),
            jnp.stack([a.reshape(kv_shape) for a in mem_v32]),
            jnp.stack(gp_["new_a"]), jnp.stack(gp_["new_b"]), jnp.stack(gp_["new_c"]),
            jnp.stack(gs_["new_a"]), jnp.stack(gs_["new_b"]), jnp.stack(gs_["new_c"]))
```
